```python
import math
import jax, jax.numpy as jnp
from jax import lax
import numpy as np

D_MODEL = 1024
BATCH = 4
SEQ = 4096
DEPTH = 4
DEC_BATCH = 32
DEC_SEQ = 4
PAST_LEN = 8192
PAGE_SIZE = 128

N_BRANCH = 3
SSM_WIDTH = D_MODEL // 2
SSM_GROUP = 16
SSM_GROUPS = SSM_WIDTH // SSM_GROUP
SSM_STATE = 64
SSM_DT_MIN = 1e-3
SSM_DT_MAX = 1e-1
SB_HEADS = 8
SB_HEAD_DIM = 64
SB_WIDTH = SB_HEADS * SB_HEAD_DIM
Q_BLOCK = 128
SB_BIAS_INIT = -8.0
GDN_HEADS = 4
GDN_HEAD_DIM = 128
GDN_WIDTH = GDN_HEADS * GDN_HEAD_DIM
GDN_CONV = 4
GDN_CHUNK = 64
GDN_DT_MIN = 1e-3
GDN_DT_MAX = 1e-1
MIX_WIDTH = SSM_WIDTH + SB_WIDTH + GDN_WIDTH
IN_SPLITS = (SSM_WIDTH, SB_WIDTH, SB_WIDTH, SB_WIDTH, 3 * GDN_WIDTH, GDN_WIDTH, GDN_HEADS, GDN_HEADS, N_BRANCH * D_MODEL)
IN_WIDTH = SSM_WIDTH + 3 * SB_WIDTH + 4 * GDN_WIDTH + 2 * GDN_HEADS + N_BRANCH * D_MODEL
D_FF = 2816
POOL_NUM = 5
POOL_DEN = 4
EPS = 1e-6

kernel_name = 'hybrid_s5_stickbreak_gdn_decoder_step'


def rmsnorm(x, g):
    xf = x.astype(jnp.float32)
    y = xf * lax.rsqrt(jnp.mean(xf * xf, axis=-1, keepdims=True) + EPS)
    return (y * g.astype(jnp.float32)).astype(x.dtype)


def modulate(h, shift, scale):
    return h * (1 + scale[:, None, :]) + shift[:, None, :]


def swiglu(h, w_gu, w_down):
    gate, up = jnp.split(h @ w_gu, 2, axis=-1)
    return (jax.nn.silu(gate) * up) @ w_down


def split_cols(x, sizes):
    idx = []
    acc = 0
    for s in sizes[:-1]:
        acc += s
        idx.append(acc)
    return jnp.split(x, idx, axis=-1)


def cmul(ar, ai, br, bi):
    return ar * br - ai * bi, ar * bi + ai * br


def s5_branch(u, lam_re, lam_im, log_dt, b_re, b_im, c_re, c_im, d_skip, w_glu, h0_re, h0_im):
    f32 = jnp.float32
    bsz, t, _ = u.shape
    uf = u.astype(f32)
    ug = uf.reshape(bsz, t, SSM_GROUPS, SSM_GROUP)
    lr = lam_re.astype(f32)
    li = lam_im.astype(f32)
    dt = jnp.exp(log_dt.astype(f32))[:, None]
    mag = jnp.exp(lr * dt)
    abr = mag * jnp.cos(li * dt)
    abi = mag * jnp.sin(li * dt)
    den = lr * lr + li * li
    fr = ((abr - 1) * lr + abi * li) / den
    fi = (abi * lr - (abr - 1) * li) / den
    bbr, bbi = cmul(fr[..., None], fi[..., None], b_re.astype(f32), b_im.astype(f32))
    bu_re = jnp.einsum('btgh,gph->btgp', ug, bbr)
    bu_im = jnp.einsum('btgh,gph->btgp', ug, bbi)
    cr, ci = cmul(abr, abi, h0_re.astype(f32), h0_im.astype(f32))
    bu_re = bu_re.at[:, 0].add(cr)
    bu_im = bu_im.at[:, 0].add(ci)
    a_re = jnp.broadcast_to(abr, bu_re.shape)
    a_im = jnp.broadcast_to(abi, bu_im.shape)

    def combine(e1, e2):
        a1r, a1i, b1r, b1i = e1
        a2r, a2i, b2r, b2i = e2
        ar, ai = cmul(a2r, a2i, a1r, a1i)
        br, bi = cmul(a2r, a2i, b1r, b1i)
        return ar, ai, br + b2r, bi + b2i

    _, _, hr, hi = lax.associative_scan(combine, (a_re, a_im, bu_re, bu_im), axis=1)
    y = jnp.einsum('btgp,ghp->btgh', hr, c_re.astype(f32)) - jnp.einsum('btgp,ghp->btgh', hi, c_im.astype(f32))
    y = y.reshape(bsz, t, SSM_WIDTH) + d_skip.astype(f32) * uf
    y = jax.nn.gelu(y).astype(u.dtype)
    out = y * jax.nn.sigmoid(y @ w_glu)
    return out, hr[:, -1], hi[:, -1]


def stick_break_block(q, k, v, q_pos, k_pos, bias):
    z = jnp.einsum('bqhd,bkhd->bhqk', q, k, preferred_element_type=jnp.float32) * (SB_HEAD_DIM ** -0.5)
    z = z + bias.astype(jnp.float32)[None, :, None, None]
    causal = k_pos[None, :] < q_pos[:, None]
    log_keep = jnp.where(causal, jax.nn.log_sigmoid(-z), 0.0)
    later = lax.cumsum(log_keep, axis=3, reverse=True) - log_keep
    w = jnp.where(causal, jnp.exp(jax.nn.log_sigmoid(z) + later), 0.0)
    return jnp.einsum('bhqk,bkhd->bqhd', w.astype(v.dtype), v)


def sb_attend(q, k, v, q_pos, k_pos, bias):
    tq = q.shape[1]
    if tq <= Q_BLOCK:
        return stick_break_block(q, k, v, q_pos, k_pos, bias)
    pad = (-tq) % Q_BLOCK
    if pad:
        q = jnp.pad(q, ((0, 0), (0, pad), (0, 0), (0, 0)))
        q_pos = jnp.concatenate([q_pos, jnp.full((pad,), q_pos[-1], q_pos.dtype)])
    nb = q.shape[1] // Q_BLOCK
    qb = jnp.moveaxis(q.reshape(q.shape[0], nb, Q_BLOCK, SB_HEADS, SB_HEAD_DIM), 1, 0)
    pb = q_pos.reshape(nb, Q_BLOCK)
    ob = lax.map(lambda a: stick_break_block(a[0], k, v, a[1], k_pos, bias), (qb, pb))
    o = jnp.moveaxis(ob, 0, 1).reshape(q.shape[0], nb * Q_BLOCK, SB_HEADS, SB_HEAD_DIM)
    return o[:, :tq]


def l2norm(x):
    return x * lax.rsqrt(jnp.sum(x * x, axis=-1, keepdims=True) + EPS)


def chunk_gated_delta(q, k, v, g, beta, s0):
    f32 = jnp.float32
    bsz, t, nh, dk = k.shape
    dv = v.shape[-1]
    cs = min(GDN_CHUNK, t)
    pad = (-t) % cs
    q, k, v, g, beta = [jnp.moveaxis(a.astype(f32), 1, 2) for a in (q, k, v, g, beta)]
    if pad:
        padt = lambda a: jnp.pad(a, [(0, 0), (0, 0), (0, pad)] + [(0, 0)] * (a.ndim - 3))
        q, k, v, g, beta = [padt(a) for a in (q, k, v, g, beta)]
    n = (t + pad) // cs
    q = q.reshape(bsz, nh, n, cs, dk)
    k = k.reshape(bsz, nh, n, cs, dk)
    v = v.reshape(bsz, nh, n, cs, dv)
    g = g.reshape(bsz, nh, n, cs)
    beta = beta.reshape(bsz, nh, n, cs)
    gc = jnp.cumsum(g, axis=-1)
    kb = k * beta[..., None]
    vb = v * beta[..., None]
    incl = jnp.tril(jnp.ones((cs, cs), dtype=bool))
    strict = jnp.tril(jnp.ones((cs, cs), dtype=bool), -1)
    diff = gc[..., :, None] - gc[..., None, :]
    decay = jnp.where(incl, jnp.exp(jnp.where(incl, diff, 0.0)), 0.0)
    m = jnp.where(strict, jnp.einsum('bhnid,bhnjd->bhnij', kb, k) * decay, 0.0)
    tri = m + jnp.eye(cs, dtype=f32)
    u = lax.linalg.triangular_solve(tri, vb, left_side=True, lower=True, unit_diagonal=True)
    w = lax.linalg.triangular_solve(tri, kb * jnp.exp(gc)[..., None], left_side=True, lower=True, unit_diagonal=True)
    qk = jnp.where(incl, jnp.einsum('bhnid,bhnjd->bhnij', q, k) * decay, 0.0)

    def step(s, xs):
        q_i, k_i, u_i, w_i, g_i, qk_i = xs
        v_new = u_i - jnp.einsum('bhck,bhkv->bhcv', w_i, s)
        o = jnp.einsum('bhck,bhkv->bhcv', q_i * jnp.exp(g_i)[..., None], s) + jnp.einsum('bhij,bhjv->bhiv', qk_i, v_new)
        g_last = g_i[..., -1:]
        s = s * jnp.exp(g_last)[..., None] + jnp.einsum('bhck,bhcv->bhkv', k_i * jnp.exp(g_last - g_i)[..., None], v_new)
        return s, o

    xs = tuple(jnp.moveaxis(a, 2, 0) for a in (q, k, u, w, gc, qk))
    s_fin, o = lax.scan(step, s0.astype(f32), xs)
    o = jnp.moveaxis(o, 0, 2).reshape(bsz, nh, n * cs, dv)[:, :, :t]
    return jnp.moveaxis(o, 1, 2), s_fin


def gdn_branch(qkv_raw, z, b_raw, a_raw, conv_w, a_log, dt_bias, norm_g, conv_buf, s0):
    f32 = jnp.float32
    bsz, t, _ = qkv_raw.shape
    xpad = jnp.concatenate([conv_buf.astype(qkv_raw.dtype), qkv_raw], axis=1)
    conv = xpad[:, 0:t] * conv_w[0]
    for i in range(1, GDN_CONV):
        conv = conv + xpad[:, i:i + t] * conv_w[i]
    new_buf = xpad[:, t:]
    qkv = jax.nn.silu(conv)
    q, k, v = [a.reshape(bsz, t, GDN_HEADS, GDN_HEAD_DIM) for a in jnp.split(qkv, 3, axis=-1)]
    q = l2norm(q.astype(f32)) * (GDN_HEAD_DIM ** -0.5)
    k = l2norm(k.astype(f32))
    beta = jax.nn.sigmoid(b_raw.astype(f32))
    g = -jnp.exp(a_log.astype(f32)) * jax.nn.softplus(a_raw.astype(f32) + dt_bias.astype(f32))
    o, s_fin = chunk_gated_delta(q, k, v, g, beta, s0)
    o = rmsnorm(o, norm_g) * jax.nn.silu(z.astype(f32).reshape(bsz, t, GDN_HEADS, GDN_HEAD_DIM))
    return o.reshape(bsz, t, GDN_WIDTH).astype(qkv_raw.dtype), new_buf, s_fin


def token_mixer(h, p, past_k, past_v, ssm_re, ssm_im, conv_buf, s_delta):
    bsz, t, _ = h.shape
    u_a, q_b, k_b, v_b, qkv_c, z_c, b_c, a_c, gate_logit = split_cols(h @ p['w_in'], IN_SPLITS)
    y_a, ssm_re_new, ssm_im_new = s5_branch(u_a, p['ssm_lambda_re'], p['ssm_lambda_im'], p['ssm_log_dt'],
                                            p['ssm_b_re'], p['ssm_b_im'], p['ssm_c_re'], p['ssm_c_im'],
                                            p['ssm_d'], p['ssm_w_glu'], ssm_re, ssm_im)
    q_b = q_b.reshape(bsz, t, SB_HEADS, SB_HEAD_DIM)
    k_b = k_b.reshape(bsz, t, SB_HEADS, SB_HEAD_DIM)
    v_b = v_b.reshape(bsz, t, SB_HEADS, SB_HEAD_DIM)
    if past_k is None:
        keys, vals, past_len = k_b, v_b, 0
    else:
        past_len = past_k.shape[1]
        keys = jnp.concatenate([past_k.astype(k_b.dtype), k_b], axis=1)
        vals = jnp.concatenate([past_v.astype(v_b.dtype), v_b], axis=1)
    k_pos = jnp.arange(past_len + t)
    q_pos = past_len + jnp.arange(t)
    y_b = sb_attend(q_b, keys, vals, q_pos, k_pos, p['sb_bias']).reshape(bsz, t, SB_WIDTH)
    y_c, conv_new, s_new = gdn_branch(qkv_c, z_c, b_c, a_c, p['gdn_conv_w'], p['gdn_a_log'], p['gdn_dt_bias'],
                                      p['gdn_norm_g'], conv_buf, s_delta)
    gates = jax.nn.sigmoid(gate_logit).reshape(bsz, t, N_BRANCH, D_MODEL)
    w_a, w_b, w_c = split_cols(p['w_branch'].T, (SSM_WIDTH, SB_WIDTH, GDN_WIDTH))
    merged = (gates[:, :, 0] * (y_a @ w_a.T) + gates[:, :, 1] * (y_b @ w_b.T)
              + gates[:, :, 2] * (y_c @ w_c.T))
    return merged @ p['w_out'], (k_b, v_b, ssm_re_new, ssm_im_new, s_new, conv_new)


def run_trunk(x, c, weights, final_norm_g, ssm_re, ssm_im, s_delta, conv_buf,
              cache_k=None, cache_v=None, page_table=None):
    news = []
    cs = jax.nn.silu(c)
    bsz = x.shape[0]
    for l in range(DEPTH):
        p = {name: arr[l] for name, arr in weights.items()}
        mod = cs @ p['w_ada'] + p['b_ada']
        sh1, sc1, g1, sh2, sc2, g2, sh3, sc3, g3 = jnp.split(mod, 9, axis=-1)
        h = modulate(rmsnorm(x, p['norm_g'][0]), sh1, sc1)
        x = x + 0.5 * g1[:, None] * swiglu(h, p['w_ffn_gu'][0], p['w_ffn_down'][0])
        h = modulate(rmsnorm(x, p['norm_g'][1]), sh2, sc2)
        if page_table is None:
            pk = None
            pv = None
        else:
            n_pages = page_table.shape[1]
            pk = cache_k[l][page_table].reshape(bsz, n_pages * PAGE_SIZE, SB_HEADS, SB_HEAD_DIM)
            pv = cache_v[l][page_table].reshape(bsz, n_pages * PAGE_SIZE, SB_HEADS, SB_HEAD_DIM)
        mix, new = token_mixer(h, p, pk, pv, ssm_re[l], ssm_im[l], conv_buf[l], s_delta[l])
        x = x + g2[:, None] * mix
        h = modulate(rmsnorm(x, p['norm_g'][2]), sh3, sc3)
        x = x + 0.5 * g3[:, None] * swiglu(h, p['w_ffn_gu'][1], p['w_ffn_down'][1])
        news.append(new)
    stacked = [jnp.stack([n[i] for n in news]) for i in range(6)]
    return rmsnorm(x, final_norm_g), stacked


def setup_inputs(seed: int = 0) -> dict:
    key = jax.random.key(seed)
    keys = jax.random.split(key, 48)
    counter = [0]

    def nk():
        counter[0] += 1
        return keys[counter[0] - 1]

    f32 = jnp.float32
    nrm = lambda shape, scale: jax.random.normal(nk(), shape, f32) * scale
    n_pages = PAST_LEN // PAGE_SIZE
    n_pool = DEC_BATCH * n_pages * POOL_NUM // POOL_DEN

    x_prompt = nrm((BATCH, SEQ, D_MODEL), 1.0)
    x_sample = nrm((DEC_BATCH, DEC_SEQ, D_MODEL), 1.0)
    cache_k = nrm((DEPTH, n_pool, PAGE_SIZE, SB_HEADS, SB_HEAD_DIM), 1.0)
    cache_v = nrm((DEPTH, n_pool, PAGE_SIZE, SB_HEADS, SB_HEAD_DIM), 1.0)
    perm = jax.random.permutation(nk(), n_pool)
    page_table = perm[: DEC_BATCH * n_pages].reshape(DEC_BATCH, n_pages).astype(jnp.int32)
    state_ssm_re = nrm((DEPTH, DEC_BATCH, SSM_GROUPS, SSM_STATE), 0.1)
    state_ssm_im = nrm((DEPTH, DEC_BATCH, SSM_GROUPS, SSM_STATE), 0.1)
    state_delta = nrm((DEPTH, DEC_BATCH, GDN_HEADS, GDN_HEAD_DIM, GDN_HEAD_DIM), GDN_HEAD_DIM ** -0.5)
    state_conv = nrm((DEPTH, DEC_BATCH, GDN_CONV - 1, 3 * GDN_WIDTH), 1.0)
    c_prompt = nrm((BATCH, D_MODEL), 1.0)
    c_sample = nrm((DEC_BATCH, D_MODEL), 1.0)

    w_ada = nrm((DEPTH, D_MODEL, 9 * D_MODEL), 0.5 * D_MODEL ** -0.5)
    b_ada = nrm((DEPTH, 9 * D_MODEL), 0.02)
    norm_g = 1.0 + nrm((DEPTH, 3, D_MODEL), 0.02)
    w_ffn_gu = nrm((DEPTH, 2, D_MODEL, 2 * D_FF), D_MODEL ** -0.5)
    w_ffn_down = nrm((DEPTH, 2, D_FF, D_MODEL), D_FF ** -0.5)
    w_in = nrm((DEPTH, D_MODEL, IN_WIDTH), D_MODEL ** -0.5)
    sb_bias = SB_BIAS_INIT + nrm((DEPTH, SB_HEADS), 0.1)
    ssm_lambda_re = -0.5 + nrm((DEPTH, SSM_GROUPS, SSM_STATE), 0.01)
    ssm_lambda_im = math.pi * jnp.arange(SSM_STATE, dtype=f32) + nrm((DEPTH, SSM_GROUPS, SSM_STATE), 0.01)
    ssm_log_dt = jax.random.uniform(nk(), (DEPTH, SSM_GROUPS), f32, math.log(SSM_DT_MIN), math.log(SSM_DT_MAX))
    ssm_b_re = nrm((DEPTH, SSM_GROUPS, SSM_STATE, SSM_GROUP), (2 * SSM_GROUP) ** -0.5)
    ssm_b_im = nrm((DEPTH, SSM_GROUPS, SSM_STATE, SSM_GROUP), (2 * SSM_GROUP) ** -0.5)
    ssm_c_re = nrm((DEPTH, SSM_GROUPS, SSM_GROUP, SSM_STATE), SSM_STATE ** -0.5)
    ssm_c_im = nrm((DEPTH, SSM_GROUPS, SSM_GROUP, SSM_STATE), SSM_STATE ** -0.5)
    ssm_d = nrm((DEPTH, SSM_WIDTH), 1.0)
    ssm_w_glu = nrm((DEPTH, SSM_WIDTH, SSM_WIDTH), SSM_WIDTH ** -0.5)
    gdn_conv_w = nrm((DEPTH, GDN_CONV, 3 * GDN_WIDTH), GDN_CONV ** -0.5)
    gdn_a_log = jnp.log(jax.random.uniform(nk(), (DEPTH, GDN_HEADS), f32, 1.0, 16.0))
    gdn_dt = jnp.exp(jax.random.uniform(nk(), (DEPTH, GDN_HEADS), f32, math.log(GDN_DT_MIN), math.log(GDN_DT_MAX)))
    gdn_dt_bias = gdn_dt + jnp.log(-jnp.expm1(-gdn_dt))
    gdn_norm_g = 1.0 + nrm((DEPTH, GDN_HEAD_DIM), 0.02)
    w_branch = nrm((DEPTH, MIX_WIDTH, D_MODEL), (MIX_WIDTH // N_BRANCH) ** -0.5)
    w_out = nrm((DEPTH, D_MODEL, D_MODEL), D_MODEL ** -0.5)
    final_norm_g = 1.0 + nrm((D_MODEL,), 0.02)
    return {
        'x_prompt': x_prompt, 'x_sample': x_sample,
        'cache_k': cache_k, 'cache_v': cache_v, 'page_table': page_table,
        'state_ssm_re': state_ssm_re, 'state_ssm_im': state_ssm_im,
        'state_delta': state_delta, 'state_conv': state_conv,
        'c_prompt': c_prompt, 'c_sample': c_sample,
        'w_ada': w_ada, 'b_ada': b_ada, 'norm_g': norm_g,
        'w_ffn_gu': w_ffn_gu, 'w_ffn_down': w_ffn_down, 'w_in': w_in, 'sb_bias': sb_bias,
        'ssm_lambda_re': ssm_lambda_re, 'ssm_lambda_im': ssm_lambda_im, 'ssm_log_dt': ssm_log_dt,
        'ssm_b_re': ssm_b_re, 'ssm_b_im': ssm_b_im, 'ssm_c_re': ssm_c_re, 'ssm_c_im': ssm_c_im,
        'ssm_d': ssm_d, 'ssm_w_glu': ssm_w_glu,
        'gdn_conv_w': gdn_conv_w, 'gdn_a_log': gdn_a_log, 'gdn_dt_bias': gdn_dt_bias, 'gdn_norm_g': gdn_norm_g,
        'w_branch': w_branch, 'w_out': w_out, 'final_norm_g': final_norm_g,
    }


def reference(x_prompt, x_sample, cache_k, cache_v, page_table, state_ssm_re, state_ssm_im, state_delta,
              state_conv, c_prompt, c_sample, w_ada, b_ada, norm_g, w_ffn_gu, w_ffn_down, w_in, sb_bias,
              ssm_lambda_re, ssm_lambda_im, ssm_log_dt, ssm_b_re, ssm_b_im, ssm_c_re, ssm_c_im, ssm_d,
              ssm_w_glu, gdn_conv_w, gdn_a_log, gdn_dt_bias, gdn_norm_g, w_branch, w_out, final_norm_g):
    weights = {
        'w_ada': w_ada, 'b_ada': b_ada, 'norm_g': norm_g, 'w_ffn_gu': w_ffn_gu, 'w_ffn_down': w_ffn_down,
        'w_in': w_in, 'sb_bias': sb_bias, 'ssm_lambda_re': ssm_lambda_re, 'ssm_lambda_im': ssm_lambda_im,
        'ssm_log_dt': ssm_log_dt,
        'ssm_b_re': ssm_b_re, 'ssm_b_im': ssm_b_im, 'ssm_c_re': ssm_c_re, 'ssm_c_im': ssm_c_im,
        'ssm_d': ssm_d, 'ssm_w_glu': ssm_w_glu, 'gdn_conv_w': gdn_conv_w, 'gdn_a_log': gdn_a_log,
        'gdn_dt_bias': gdn_dt_bias, 'gdn_norm_g': gdn_norm_g, 'w_branch': w_branch, 'w_out': w_out,
    }
    bp = x_prompt.shape[0]
    zeros_ssm = jnp.zeros((DEPTH, bp, SSM_GROUPS, SSM_STATE), jnp.float32)
    zeros_delta = jnp.zeros((DEPTH, bp, GDN_HEADS, GDN_HEAD_DIM, GDN_HEAD_DIM), jnp.float32)
    zeros_conv = jnp.zeros((DEPTH, bp, GDN_CONV - 1, 3 * GDN_WIDTH), x_prompt.dtype)
    y_prompt, new_p = run_trunk(x_prompt, c_prompt, weights, final_norm_g,
                                zeros_ssm, zeros_ssm, zeros_delta, zeros_conv)
    y_sample, new_s = run_trunk(x_sample, c_sample, weights, final_norm_g,
                                state_ssm_re, state_ssm_im, state_delta, state_conv,
                                cache_k, cache_v, page_table)
    k_prompt, v_prompt, ssm_re_prompt, ssm_im_prompt, delta_prompt, conv_prompt = new_p
    k_sample, v_sample, ssm_re_sample, ssm_im_sample, delta_sample, conv_sample = new_s
    return (y_prompt, y_sample, k_prompt, v_prompt, k_sample, v_sample,
            ssm_re_prompt, ssm_im_prompt, ssm_re_sample, ssm_im_sample,
            delta_prompt, delta_sample, conv_prompt, conv_sample)
```

```python
import functools
import math

import jax
import jax.numpy as jnp
from jax import lax
from jax.experimental import pallas as pl
from jax.experimental.pallas import tpu as pltpu

F32 = jnp.float32
BF16 = jnp.bfloat16

D_MODEL = 1024
DEPTH = 4
D_FF = 2816
PAGE_SIZE = 128
SSM_WIDTH = 512
SSM_GROUP = 16
SSM_GROUPS = 32
SSM_STATE = 64
SSM_LANES = SSM_GROUPS * SSM_STATE
SB_HEADS = 8
SB_HEAD_DIM = 64
SB_WIDTH = 512
GDN_HEADS = 4
GDN_HEAD_DIM = 128
GDN_WIDTH = 512
GDN_CONV = 4
GDN_CHUNK = 64
EPS = 1e-6

LANE = 128
SUBLANE = 8
VMEM_LIMIT = 48 * 1024 * 1024

CB_GATE, CB_U, CB_Q, CB_K, CB_V, CB_QKVC, CB_Z, CB_BA = 0, 6, 7, 8, 9, 10, 13, 14
PROJ_W = 15 * 512

S5_STREAMS = 8
S5_HALF = 2
NT_DIMS = (((1,), (1,)), ((), ()))
TN_DIMS = (((0,), (0,)), ((), ()))


def _cparams(sem, vmem=VMEM_LIMIT):
    return pltpu.CompilerParams(dimension_semantics=sem, vmem_limit_bytes=vmem)


def _dot(a, b):
    return jnp.dot(a, b, preferred_element_type=F32)


def _dot_f32(a, b):
    return jnp.dot(a, b, precision=lax.Precision.HIGHEST, preferred_element_type=F32)


def _split3(a):
    hi = a.astype(BF16)
    r = a - hi.astype(F32)
    mid = r.astype(BF16)
    lo = (r - mid.astype(F32)).astype(BF16)
    return hi, mid, lo


def _dot_exact_rhs(a, m):
    hi, mid, lo = _split3(a)
    return _dot(hi, m) + _dot(mid, m) + _dot(lo, m)


def _sigmoid(x):
    return jax.nn.sigmoid(x)


def _silu(x):
    return x * jax.nn.sigmoid(x)


def _norm_mod(x, g, shift, scale):
    ms = jnp.mean(x * x, axis=-1, keepdims=True)
    y = (x * lax.rsqrt(ms + EPS)) * g
    return y * (1.0 + scale) + shift


def _mod_spec(layer, rows, chunk, tiles_per_group):
    return pl.BlockSpec((None, None, rows, D_MODEL), lambda i, *_: (layer, i // tiles_per_group, 0, chunk))


def _ada_kernel(c_ref, w_ref, b_ref, o_ref):
    cs = _silu(c_ref[...]).astype(BF16)
    o_ref[...] = _dot(cs, w_ref[...].astype(BF16)) + b_ref[...]


def ada_mod(c_all, w_ada, b_ada, tn=1152):
    m = c_all.shape[0]
    depth, _, n = w_ada.shape
    return pl.pallas_call(
        _ada_kernel,
        grid=(depth, n // tn),
        in_specs=[pl.BlockSpec((m, D_MODEL), lambda l, j: (0, 0)),
                  pl.BlockSpec((None, D_MODEL, tn), lambda l, j: (l, 0, j)),
                  pl.BlockSpec((None, 1, tn), lambda l, j: (l, 0, j))],
        out_specs=pl.BlockSpec((None, m, tn), lambda l, j: (l, 0, j)),
        out_shape=jax.ShapeDtypeStruct((depth, m, n), F32),
        compiler_params=_cparams(("parallel", "parallel")),
        name="ada_mod",
    )(c_all, w_ada, b_ada.reshape(depth, 1, n))


def _ffn_kernel(x_ref, g_ref, sh_ref, sc_ref, gt_ref, wg_ref, wu_ref, wd_ref, o_ref, h_scr, acc_scr):
    f = pl.program_id(1)

    @pl.when(f == 0)
    def _():
        h_scr[...] = _norm_mod(x_ref[...], g_ref[...], sh_ref[...], sc_ref[...]).astype(BF16)
        acc_scr[...] = jnp.zeros_like(acc_scr)

    h = h_scr[...]
    gate = _dot(h, wg_ref[...])
    up = _dot(h, wu_ref[...])
    act = (_silu(gate) * up).astype(BF16)
    acc_scr[...] += _dot(act, wd_ref[...])

    @pl.when(f == pl.num_programs(1) - 1)
    def _():
        o_ref[...] = x_ref[...] + 0.5 * gt_ref[...] * acc_scr[...]


def ffn(x, mod, norm_g, w_gu, w_down, *, layer, which, tm, mod_rows, tiles_per_group, tf=256):
    n = x.shape[0]
    nf = D_FF // tf
    base = 0 if which == 0 else 6
    return pl.pallas_call(
        _ffn_kernel,
        grid=(n // tm, nf),
        in_specs=[pl.BlockSpec((tm, D_MODEL), lambda i, f: (i, 0)),
                  pl.BlockSpec((None, None, 1, D_MODEL), lambda i, f: (layer, 0 if which == 0 else 2, 0, 0)),
                  _mod_spec(layer, mod_rows, base + 0, tiles_per_group),
                  _mod_spec(layer, mod_rows, base + 1, tiles_per_group),
                  _mod_spec(layer, mod_rows, base + 2, tiles_per_group),
                  pl.BlockSpec((None, None, D_MODEL, tf), lambda i, f: (layer, which, 0, f)),
                  pl.BlockSpec((None, None, D_MODEL, tf), lambda i, f: (layer, which, 0, nf + f)),
                  pl.BlockSpec((None, None, tf, D_MODEL), lambda i, f: (layer, which, f, 0))],
        out_specs=pl.BlockSpec((tm, D_MODEL), lambda i, f: (i, 0)),
        out_shape=jax.ShapeDtypeStruct((n, D_MODEL), F32),
        scratch_shapes=[pltpu.VMEM((tm, D_MODEL), BF16), pltpu.VMEM((tm, D_MODEL), F32)],
        compiler_params=_cparams(("parallel", "arbitrary")),
        name="ffn",
    )(x, norm_g, mod, mod, mod, w_gu, w_gu, w_down)


def _inproj_kernel(x_ref, g_ref, sh_ref, sc_ref, w_ref, o_ref, h_scr):
    @pl.when(pl.program_id(1) == 0)
    def _():
        h_scr[...] = _norm_mod(x_ref[...], g_ref[...], sh_ref[...], sc_ref[...]).astype(BF16)

    o_ref[...] = _dot(h_scr[...], w_ref[...])


def in_proj(x, mod, norm_g, w_in_p, *, layer, tm, mod_rows, tiles_per_group, tn=512):
    n = x.shape[0]
    return pl.pallas_call(
        _inproj_kernel,
        grid=(n // tm, PROJ_W // tn),
        in_specs=[pl.BlockSpec((tm, D_MODEL), lambda i, j: (i, 0)),
                  pl.BlockSpec((None, None, 1, D_MODEL), lambda i, j: (layer, 1, 0, 0)),
                  _mod_spec(layer, mod_rows, 3, tiles_per_group),
                  _mod_spec(layer, mod_rows, 4, tiles_per_group),
                  pl.BlockSpec((None, D_MODEL, tn), lambda i, j: (layer, 0, j))],
        out_specs=pl.BlockSpec((tm, tn), lambda i, j: (i, j)),
        out_shape=jax.ShapeDtypeStruct((n, PROJ_W), F32),
        scratch_shapes=[pltpu.VMEM((tm, D_MODEL), BF16)],
        compiler_params=_cparams(("parallel", "arbitrary")),
        name="in_proj",
    )(x, norm_g, mod, mod, w_in_p)


def _merge_kernel(x_ref, gl_ref, ya_ref, yb_ref, yc_ref, gt_ref, wb_ref, wo_ref, o_ref):
    merged = None
    for b, y_ref in enumerate((ya_ref, yb_ref, yc_ref)):
        proj = _dot(y_ref[...].astype(BF16), wb_ref[512 * b:512 * (b + 1), :])
        term = _sigmoid(gl_ref[:, D_MODEL * b:D_MODEL * (b + 1)]) * proj
        merged = term if merged is None else merged + term
    mix = _dot(merged.astype(BF16), wo_ref[...])
    o_ref[...] = x_ref[...] + gt_ref[...] * mix


def merge_out(x, proj, y_a, y_b, y_c, mod, w_branch, w_out, *, layer, tm, mod_rows, tiles_per_group):
    n = x.shape[0]
    gate_w = 3 * D_MODEL
    return pl.pallas_call(
        _merge_kernel,
        grid=(n // tm,),
        in_specs=[pl.BlockSpec((tm, D_MODEL), lambda i: (i, 0)),
                  pl.BlockSpec((tm, gate_w), lambda i: (i, CB_GATE * 512 // gate_w)),
                  pl.BlockSpec((tm, 512), lambda i: (i, 0)),
                  pl.BlockSpec((tm, 512), lambda i: (i, 0)),
                  pl.BlockSpec((tm, 512), lambda i: (i, 0)),
                  _mod_spec(layer, mod_rows, 5, tiles_per_group),
                  pl.BlockSpec((None, 3 * 512, D_MODEL), lambda i: (layer, 0, 0)),
                  pl.BlockSpec((None, D_MODEL, D_MODEL), lambda i: (layer, 0, 0))],
        out_specs=pl.BlockSpec((tm, D_MODEL), lambda i: (i, 0)),
        out_shape=jax.ShapeDtypeStruct((n, D_MODEL), F32),
        compiler_params=_cparams(("parallel",)),
        name="merge_out",
    )(x, proj, y_a, y_b, y_c, mod, w_branch, w_out)


def _final_norm_kernel(x_ref, g_ref, o_ref):
    x = x_ref[...]
    ms = jnp.mean(x * x, axis=-1, keepdims=True)
    o_ref[...] = (x * lax.rsqrt(ms + EPS)) * g_ref[...]


def final_norm(x, g, tm):
    n = x.shape[0]
    return pl.pallas_call(
        _final_norm_kernel,
        grid=(n // tm,),
        in_specs=[pl.BlockSpec((tm, D_MODEL), lambda i: (i, 0)),
                  pl.BlockSpec((1, D_MODEL), lambda i: (0, 0))],
        out_specs=pl.BlockSpec((tm, D_MODEL), lambda i: (i, 0)),
        out_shape=jax.ShapeDtypeStruct((n, D_MODEL), F32),
        compiler_params=_cparams(("parallel",)),
        name="final_norm",
    )(x, g.reshape(1, D_MODEL))


def s5_discretize(lam_re, lam_im, log_dt, b_re, b_im, c_re, c_im):
    lr = lam_re.astype(F32)
    li = lam_im.astype(F32)
    dt = jnp.exp(log_dt.astype(F32))[:, None]
    mag = jnp.exp(lr * dt)
    abr = mag * jnp.cos(li * dt)
    abi = mag * jnp.sin(li * dt)
    den = lr * lr + li * li
    fr = ((abr - 1) * lr + abi * li) / den
    fi = (abi * lr - (abr - 1) * li) / den
    bbr = fr[..., None] * b_re - fi[..., None] * b_im
    bbi = fr[..., None] * b_im + fi[..., None] * b_re
    gh = SSM_GROUPS // S5_HALF
    eye = jnp.eye(gh, dtype=F32)

    def blk_in(bb):
        bb4 = bb.reshape(S5_HALF, gh, SSM_STATE, SSM_GROUP)
        return jnp.einsum('ogph,gk->oghkp', bb4, eye).reshape(S5_HALF, gh * SSM_GROUP, gh * SSM_STATE)

    def blk_out(c):
        c4 = c.reshape(S5_HALF, gh, SSM_GROUP, SSM_STATE)
        return jnp.einsum('oghp,gk->ogpkh', c4, eye).reshape(S5_HALF, gh * SSM_STATE, gh * SSM_GROUP)

    bblk = jnp.concatenate([blk_in(bbr), blk_in(bbi)], axis=-1).astype(BF16)
    cblk = jnp.concatenate([blk_out(c_re.astype(F32)), -blk_out(c_im.astype(F32))], axis=1).astype(BF16)
    return abr.reshape(1, SSM_LANES), abi.reshape(1, SSM_LANES), bblk, cblk


def _s5_kernel(u_ref, h0r_ref, h0i_ref, ar_ref, ai_ref, bblk_ref, cblk_ref, d_ref, wglu_ref,
               *rest, lj, cap_idx, emit_y):
    if emit_y:
        y_ref, hfr_ref, hfi_ref, bu_scr, st_scr, h_scr = rest
    else:
        hfr_ref, hfi_ref, bu_scr, st_scr = rest
    jc = pl.program_id(1)
    rows = S5_STREAMS * lj
    half_lanes = SSM_LANES // S5_HALF
    tiles_half = half_lanes // LANE
    n_re_tiles = SSM_LANES // LANE

    @pl.when(jc == 0)
    def _():
        st_scr[0] = h0r_ref[...]
        st_scr[1] = h0i_ref[...]

    u2 = u_ref[...].reshape(rows, SSM_WIDTH)
    ub = u2.astype(BF16)
    hw = SSM_WIDTH // S5_HALF
    for o in range(S5_HALF):
        bu = _dot(ub[:, hw * o:hw * (o + 1)], bblk_ref[o])
        for t in range(tiles_half):
            bu_scr[tiles_half * o + t] = bu[:, LANE * t:LANE * (t + 1)]
            bu_scr[n_re_tiles + tiles_half * o + t] = bu[:, half_lanes + LANE * t:half_lanes + LANE * (t + 1)]

    cw = 4
    for c0 in range(0, n_re_tiles, cw):
        sl = slice(LANE * c0, LANE * (c0 + cw))
        ar = jnp.broadcast_to(ar_ref[:, sl], (S5_STREAMS, LANE * cw))
        ai = jnp.broadcast_to(ai_ref[:, sl], (S5_STREAMS, LANE * cw))
        hr = st_scr[0, :, sl]
        hi = st_scr[1, :, sl]
        for jj in range(lj):
            idx = pl.ds(jj, S5_STREAMS, stride=lj)
            bur = jnp.concatenate([bu_scr[c0 + t, idx, :] for t in range(cw)], axis=1)
            bui = jnp.concatenate([bu_scr[n_re_tiles + c0 + t, idx, :] for t in range(cw)], axis=1)
            hr, hi = ar * hr - ai * hi + bur, ar * hi + ai * hr + bui
            if emit_y:
                for t in range(cw):
                    h_scr[c0 + t, idx, :] = hr[:, LANE * t:LANE * (t + 1)]
                    h_scr[n_re_tiles + c0 + t, idx, :] = hi[:, LANE * t:LANE * (t + 1)]
            if jj == cap_idx:
                hfr_ref[:, sl] = hr
                hfi_ref[:, sl] = hi
        st_scr[0, :, sl] = hr
        st_scr[1, :, sl] = hi

    if emit_y:
        ys = []
        for o in range(S5_HALF):
            tiles = [h_scr[tiles_half * o + t] for t in range(tiles_half)]
            tiles += [h_scr[n_re_tiles + tiles_half * o + t] for t in range(tiles_half)]
            hcat = jnp.concatenate(tiles, axis=1).astype(BF16)
            ys.append(_dot(hcat, cblk_ref[o]))
        y = jnp.concatenate(ys, axis=1) + d_ref[...] * u2
        y = jax.nn.gelu(y)
        out = y * _sigmoid(_dot(y.astype(BF16), wglu_ref[...]))
        y_ref[...] = out.reshape(S5_STREAMS, lj, SSM_WIDTH)


def s5_scan(u3, col_block, h0r, h0i, ar, ai, bblk, cblk, d_skip, w_glu, *, lj, n_valid, emit_y):
    s, j, _ = u3.shape
    njc = j // lj
    cap_idx = (n_valid - 1) - (njc - 1) * lj
    rows = S5_STREAMS * lj
    st_spec = pl.BlockSpec((S5_STREAMS, SSM_LANES), lambda g, c: (g, 0))
    full = lambda shape: pl.BlockSpec(shape, lambda g, c: (0,) * len(shape))
    in_specs = [pl.BlockSpec((S5_STREAMS, lj, SSM_WIDTH), lambda g, c: (g, c, col_block)),
                st_spec, st_spec, full((1, SSM_LANES)), full((1, SSM_LANES)),
                full(bblk.shape), full(cblk.shape), full((1, SSM_WIDTH)), full((SSM_WIDTH, SSM_WIDTH))]
    fin_shape = jax.ShapeDtypeStruct((s, SSM_LANES), F32)
    out_specs = [st_spec, st_spec]
    out_shape = [fin_shape, fin_shape]
    scratch = [pltpu.VMEM((2 * SSM_LANES // LANE, rows, LANE), F32), pltpu.VMEM((2, S5_STREAMS, SSM_LANES), F32)]
    if emit_y:
        out_specs = [pl.BlockSpec((S5_STREAMS, lj, SSM_WIDTH), lambda g, c: (g, c, 0))] + out_specs
        out_shape = [jax.ShapeDtypeStruct((s, j, SSM_WIDTH), F32)] + out_shape
        scratch.append(pltpu.VMEM((2 * SSM_LANES // LANE, rows, LANE), F32))
    return pl.pallas_call(
        functools.partial(_s5_kernel, lj=lj, cap_idx=cap_idx, emit_y=emit_y),
        grid=(s // S5_STREAMS, njc),
        in_specs=in_specs, out_specs=out_specs, out_shape=out_shape, scratch_shapes=scratch,
        compiler_params=_cparams(("parallel", "arbitrary")),
        name="s5_scan_y" if emit_y else "s5_scan_state",
    )(u3, h0r, h0i, ar, ai, bblk, cblk, d_skip, w_glu)


def s5_combine(er, ei, h0r, h0i, ar, ai, *, nseg, seg_len):
    s = er.shape[0]
    nbatch = s // nseg
    log2_steps = int(math.log2(seg_len))
    assert 2 ** log2_steps == seg_len
    tile = lambda a: a.reshape(a.shape[0], SSM_LANES // LANE, LANE).transpose(1, 0, 2)
    untile = lambda a: a.transpose(1, 0, 2).reshape(a.shape[1], SSM_LANES)
    nt = SSM_LANES // LANE

    def kern(er_ref, ei_ref, h0r_ref, h0i_ref, ar_ref, ai_ref, ir_ref, ii_ref, fr_ref, fi_ref):
        for t in range(nt):
            pr = jnp.broadcast_to(ar_ref[t], (nbatch, LANE))
            pi = jnp.broadcast_to(ai_ref[t], (nbatch, LANE))
            for _ in range(log2_steps):
                pr, pi = pr * pr - pi * pi, 2.0 * (pr * pi)
            hr = h0r_ref[t]
            hi = h0i_ref[t]
            for seg in range(nseg):
                idx = pl.ds(seg, nbatch, stride=nseg)
                ir_ref[t, idx, :] = hr
                ii_ref[t, idx, :] = hi
                hr, hi = pr * hr - pi * hi + er_ref[t, idx, :], pr * hi + pi * hr + ei_ref[t, idx, :]
            fr_ref[t] = hr
            fi_ref[t] = hi

    seg_shape = jax.ShapeDtypeStruct((nt, s, LANE), F32)
    fin_shape = jax.ShapeDtypeStruct((nt, nbatch, LANE), F32)
    ir, ii, fr, fi = pl.pallas_call(
        kern, out_shape=[seg_shape, seg_shape, fin_shape, fin_shape], name="s5_combine",
    )(tile(er), tile(ei), tile(h0r), tile(h0i), ar.reshape(nt, 1, LANE), ai.reshape(nt, 1, LANE))
    return untile(ir), untile(ii), untile(fr), untile(fi)


def _stick_block(q_h, k_h, v_h, bias, tri, acc, carry, strict):
    z = lax.dot_general(q_h, k_h, NT_DIMS, preferred_element_type=F32) + bias
    lp = jnp.log1p(jnp.exp(-jnp.abs(z)))
    log_beta = jnp.minimum(z, 0.0) - lp
    log_keep = -jnp.maximum(z, 0.0) - lp
    if strict is not None:
        log_keep = jnp.where(strict, log_keep, 0.0)
    later = _dot_exact_rhs(log_keep, tri) + carry
    w = jnp.exp(log_beta + later)
    if strict is not None:
        w = jnp.where(strict, w, 0.0)
    acc = acc + _dot(w.astype(BF16), v_h)
    carry = carry + jnp.sum(log_keep, axis=1, keepdims=True)
    return acc, carry


def _sb_prompt_kernel(bias_ref, q_ref, k_ref, v_ref, o_ref, *, tq):
    i = pl.program_id(1)
    row = lax.broadcasted_iota(jnp.int32, (tq, tq), 0)
    col = lax.broadcasted_iota(jnp.int32, (tq, tq), 1)
    strict = col < row
    tri = jnp.where(row > col, 1.0, 0.0).astype(BF16)
    for h in range(SB_HEADS):
        hs = slice(SB_HEAD_DIM * h, SB_HEAD_DIM * (h + 1))
        q_h = (q_ref[:, hs] * (SB_HEAD_DIM ** -0.5)).astype(BF16)
        bias = bias_ref[h]

        def block(j, acc, carry, mask):
            off = pl.multiple_of(j * tq, tq)
            k_h = k_ref[pl.ds(off, tq), hs].astype(BF16)
            v_h = v_ref[pl.ds(off, tq), hs].astype(BF16)
            return _stick_block(q_h, k_h, v_h, bias, tri, acc, carry, mask)

        acc, carry = block(i, jnp.zeros((tq, SB_HEAD_DIM), F32), jnp.zeros((tq, 1), F32), strict)
        acc, carry = lax.fori_loop(1, i + 1, lambda kk, c: block(i - kk, c[0], c[1], None), (acc, carry))
        o_ref[:, hs] = acc


def sb_prompt(proj, sb_bias_l, *, nbatch, seq, tq=256):
    n = proj.shape[0]
    nq = seq // tq
    return pl.pallas_call(
        functools.partial(_sb_prompt_kernel, tq=tq),
        grid=(nbatch, nq),
        in_specs=[pl.BlockSpec(memory_space=pltpu.SMEM),
                  pl.BlockSpec((tq, SB_WIDTH), lambda b, i: (b * nq + i, CB_Q)),
                  pl.BlockSpec((seq, SB_WIDTH), lambda b, i: (b, CB_K)),
                  pl.BlockSpec((seq, SB_WIDTH), lambda b, i: (b, CB_V))],
        out_specs=pl.BlockSpec((tq, SB_WIDTH), lambda b, i: (b * nq + i, 0)),
        out_shape=jax.ShapeDtypeStruct((n, SB_WIDTH), F32),
        compiler_params=_cparams(("parallel", "arbitrary")),
        name="sb_prompt",
    )(sb_bias_l, proj, proj, proj)


SB_QPAD = 8
SB_PAGES_PER_STEP = 4


def _sb_sample_kernel(pt_ref, bias_ref, q_ref, knew_ref, vnew_ref, *rest, pages_per_step):
    k_refs = rest[:pages_per_step]
    v_refs = rest[pages_per_step:2 * pages_per_step]
    o_ref, acc_scr, carry_scr = rest[2 * pages_per_step:]
    s = pl.program_id(1)
    rows = SB_HEADS * SB_QPAD
    row = lax.broadcasted_iota(jnp.int32, (PAGE_SIZE, PAGE_SIZE), 0)
    col = lax.broadcasted_iota(jnp.int32, (PAGE_SIZE, PAGE_SIZE), 1)
    tri = jnp.where(row > col, 1.0, 0.0).astype(BF16)
    qs = [q_ref[h].astype(BF16) for h in range(SB_HEADS)]
    bias = bias_ref[...]

    def do_page(kp_ref, vp_ref, is_new):
        zs = []
        for h in range(SB_HEADS):
            k_h = kp_ref[pl.ds(h, PAGE_SIZE, stride=SB_HEADS), :].astype(BF16)
            zs.append(lax.dot_general(qs[h], k_h, NT_DIMS, preferred_element_type=F32))
        z = jnp.concatenate(zs, axis=0) + bias
        lp = jnp.log1p(jnp.exp(-jnp.abs(z)))
        log_beta = jnp.minimum(z, 0.0) - lp
        log_keep = -jnp.maximum(z, 0.0) - lp
        if is_new:
            t_idx = lax.broadcasted_iota(jnp.int32, (rows, PAGE_SIZE), 0) % SB_QPAD
            key_idx = lax.broadcasted_iota(jnp.int32, (rows, PAGE_SIZE), 1)
            strict = key_idx < t_idx
            log_keep = jnp.where(strict, log_keep, 0.0)
        later = _dot_exact_rhs(log_keep, tri) + carry_scr[...]
        w = jnp.exp(log_beta + later)
        if is_new:
            w = jnp.where(strict, w, 0.0)
        wb = w.astype(BF16)
        for h in range(SB_HEADS):
            v_h = vp_ref[pl.ds(h, PAGE_SIZE, stride=SB_HEADS), :].astype(BF16)
            rs = slice(SB_QPAD * h, SB_QPAD * (h + 1))
            acc_scr[rs, :] += _dot(wb[rs, :], v_h)
        carry_scr[...] += jnp.sum(log_keep, axis=1, keepdims=True)

    @pl.when(s == 0)
    def _():
        acc_scr[...] = jnp.zeros_like(acc_scr)
        carry_scr[...] = jnp.zeros_like(carry_scr)
        do_page(knew_ref, vnew_ref, True)

    for r in range(pages_per_step):
        do_page(k_refs[r], v_refs[r], False)

    @pl.when(s == pl.num_programs(1) - 1)
    def _():
        o_ref[...] = acc_scr[...]


def sb_sample(q_s, k_new, v_new, cache_k4, cache_v4, page_table, bias_col, *, layer):
    nb, n_pages = page_table.shape
    pps = SB_PAGES_PER_STEP
    nsteps = n_pages // pps
    rows = SB_HEADS * SB_QPAD
    prow = PAGE_SIZE * SB_HEADS

    def page_spec(r):
        return pl.BlockSpec((None, None, prow, SB_HEAD_DIM),
                            lambda b, s, pt: (layer, pt[b, n_pages - 1 - (s * pps + r)], 0, 0))

    grid_spec = pltpu.PrefetchScalarGridSpec(
        num_scalar_prefetch=1,
        grid=(nb, nsteps),
        in_specs=[pl.BlockSpec((rows, 1), lambda b, s, pt: (0, 0)),
                  pl.BlockSpec((None, SB_HEADS, SB_QPAD, SB_HEAD_DIM), lambda b, s, pt: (b, 0, 0, 0)),
                  pl.BlockSpec((None, prow, SB_HEAD_DIM), lambda b, s, pt: (b, 0, 0)),
                  pl.BlockSpec((None, prow, SB_HEAD_DIM), lambda b, s, pt: (b, 0, 0))]
                 + [page_spec(r) for r in range(pps)] + [page_spec(r) for r in range(pps)],
        out_specs=pl.BlockSpec((None, rows, SB_HEAD_DIM), lambda b, s, pt: (b, 0, 0)),
        scratch_shapes=[pltpu.VMEM((rows, SB_HEAD_DIM), F32), pltpu.VMEM((rows, 1), F32)],
    )
    return pl.pallas_call(
        functools.partial(_sb_sample_kernel, pages_per_step=pps),
        grid_spec=grid_spec,
        out_shape=jax.ShapeDtypeStruct((nb, rows, SB_HEAD_DIM), F32),
        compiler_params=_cparams(("parallel", "arbitrary")),
        name="sb_sample",
    )(page_table, bias_col, q_s, k_new, v_new, *([cache_k4] * pps), *([cache_v4] * pps))


def _gdn_kernel(qr_ref, kr_ref, vr_ref, z_ref, ba_ref, cbuf_ref, cw_ref, hp_ref, ng_ref, s0_ref,
                y_ref, sfin_ref, cnew_ref, xs_scr, s_scr, *, t_valid, nt):
    cs = GDN_CHUNK
    t = pl.program_id(1)
    hd = GDN_HEAD_DIM

    @pl.when(t == 0)
    def _():
        s_scr[...] = s0_ref[...]
        for slab in range(3):
            xs_scr[slab, 0:SUBLANE, :] = cbuf_ref[:, GDN_WIDTH * slab:GDN_WIDTH * (slab + 1)]

    qkv = []
    for slab, r_ref in enumerate((qr_ref, kr_ref, vr_ref)):
        xs_scr[slab, SUBLANE:SUBLANE + cs, :] = r_ref[...]
        ws = slice(GDN_WIDTH * slab, GDN_WIDTH * (slab + 1))
        conv = xs_scr[slab, pl.ds(SUBLANE - 3, cs), :] * cw_ref[0:1, ws]
        for tap in range(1, GDN_CONV):
            conv = conv + xs_scr[slab, pl.ds(SUBLANE - 3 + tap, cs), :] * cw_ref[tap:tap + 1, ws]
        qkv.append(_silu(conv))

    row_id = lax.broadcasted_iota(jnp.int32, (cs, LANE), 0) + t * cs
    valid = row_id < t_valid
    ba = ba_ref[:, 0:LANE]
    beta_all = jnp.where(valid, _sigmoid(ba), 0.0)
    g_all = jnp.where(valid, -jnp.exp(hp_ref[0:1, :]) * jax.nn.softplus(ba + hp_ref[1:2, :]), 0.0)

    ri = lax.broadcasted_iota(jnp.int32, (cs, cs), 0)
    ci = lax.broadcasted_iota(jnp.int32, (cs, cs), 1)
    incl = ci <= ri
    strict = ci < ri
    eye = jnp.where(ci == ri, 1.0, 0.0).astype(F32)
    tril = jnp.where(incl, 1.0, 0.0).astype(F32)
    gc_all = _dot_f32(tril, g_all)
    gc_t = gc_all.T

    for h in range(GDN_HEADS):
        hs = slice(hd * h, hd * (h + 1))
        q = qkv[0][:, hs]
        k = qkv[1][:, hs]
        v = qkv[2][:, hs]
        q = q * lax.rsqrt(jnp.sum(q * q, axis=-1, keepdims=True) + EPS) * (hd ** -0.5)
        k = k * lax.rsqrt(jnp.sum(k * k, axis=-1, keepdims=True) + EPS)
        beta = beta_all[:, h:h + 1]
        gc = gc_all[:, GDN_HEADS + h:GDN_HEADS + h + 1]
        gc_row = gc_t[GDN_HEADS + h:GDN_HEADS + h + 1, :]
        g_last = gc_row[:, cs - 1:cs]
        kb = k * beta
        vb = v * beta
        kbf = k.astype(BF16)
        diff = gc - gc_row
        decay = jnp.where(incl, jnp.exp(jnp.where(incl, diff, 0.0)), 0.0)
        m = jnp.where(strict, lax.dot_general(kb.astype(BF16), kbf, NT_DIMS, preferred_element_type=F32) * decay, 0.0)
        n_pow = -m
        inv = eye + n_pow
        for _ in range(int(math.log2(cs)) - 1):
            n_pow = _dot_f32(n_pow, n_pow)
            inv = inv + _dot_f32(inv, n_pow)
        u = _dot_f32(inv, vb)
        w = _dot_f32(inv, kb * jnp.exp(gc))
        qk = jnp.where(incl, lax.dot_general(q.astype(BF16), kbf, NT_DIMS, preferred_element_type=F32) * decay, 0.0)

        s_old = s_scr[h]
        sb = s_old.astype(BF16)
        v_new = u - _dot(w.astype(BF16), sb)
        vnb = v_new.astype(BF16)
        o = _dot((q * jnp.exp(gc)).astype(BF16), sb) + _dot(qk.astype(BF16), vnb)
        kd = (k * jnp.exp(g_last - gc)).astype(BF16)
        s_scr[h] = s_old * jnp.exp(g_last) + lax.dot_general(kd, vnb, TN_DIMS, preferred_element_type=F32)

        on = o * lax.rsqrt(jnp.mean(o * o, axis=-1, keepdims=True) + EPS) * ng_ref[...]
        y_ref[:, hs] = on * _silu(z_ref[:, hs])

    tail = t_valid - (nt - 1) * cs
    for slab in range(3):
        @pl.when(t == nt - 1)
        def _(slab=slab):
            cnew_ref[:, GDN_WIDTH * slab:GDN_WIDTH * (slab + 1)] = xs_scr[slab, pl.ds(tail, SUBLANE), :]
        xs_scr[slab, 0:SUBLANE, :] = xs_scr[slab, cs:cs + SUBLANE, :]

    @pl.when(t == nt - 1)
    def _():
        sfin_ref[...] = s_scr[...]


def gdn(x3, col0, cbuf8, conv_w, head_params, norm_g, s0, *, t_valid):
    ns, tt, _ = x3.shape
    cs = GDN_CHUNK
    nt = tt // cs
    blk = lambda c: pl.BlockSpec((None, cs, GDN_WIDTH), lambda s, t: (s, t, c))
    kern = functools.partial(_gdn_kernel, t_valid=t_valid, nt=nt)
    return pl.pallas_call(
        kern,
        grid=(ns, nt),
        in_specs=[blk(col0), blk(col0 + 1), blk(col0 + 2), blk(col0 + 3), blk(col0 + 4),
                  pl.BlockSpec((None, SUBLANE, 3 * GDN_WIDTH), lambda s, t: (s, 0, 0)),
                  pl.BlockSpec((GDN_CONV, 3 * GDN_WIDTH), lambda s, t: (0, 0)),
                  pl.BlockSpec((2, LANE), lambda s, t: (0, 0)),
                  pl.BlockSpec((1, GDN_HEAD_DIM), lambda s, t: (0, 0)),
                  pl.BlockSpec((None, GDN_HEADS, GDN_HEAD_DIM, GDN_HEAD_DIM), lambda s, t: (s, 0, 0, 0))],
        out_specs=[pl.BlockSpec((None, cs, GDN_WIDTH), lambda s, t: (s, t, 0)),
                   pl.BlockSpec((None, GDN_HEADS, GDN_HEAD_DIM, GDN_HEAD_DIM), lambda s, t: (s, 0, 0, 0)),
                   pl.BlockSpec((None, SUBLANE, 3 * GDN_WIDTH), lambda s, t: (s, 0, 0))],
        out_shape=[jax.ShapeDtypeStruct((ns, tt, GDN_WIDTH), F32),
                   jax.ShapeDtypeStruct((ns, GDN_HEADS, GDN_HEAD_DIM, GDN_HEAD_DIM), F32),
                   jax.ShapeDtypeStruct((ns, SUBLANE, 3 * GDN_WIDTH), F32)],
        scratch_shapes=[pltpu.VMEM((3, cs + 2 * SUBLANE, GDN_WIDTH), F32),
                        pltpu.VMEM((GDN_HEADS, GDN_HEAD_DIM, GDN_HEAD_DIM), F32)],
        compiler_params=_cparams(("parallel", "arbitrary")),
        name="gdn",
    )(x3, x3, x3, x3, x3, cbuf8, conv_w, head_params, norm_g, s0)


S5_SEGMENTS = 8
S5_STEPS_PER_CALL = 64


def _col(proj, cb, width=512):
    return proj[:, cb * 512:cb * 512 + width]


def _layer_params(layer, w):
    ar, ai, bblk, cblk = s5_discretize(w['ssm_lambda_re'][layer], w['ssm_lambda_im'][layer], w['ssm_log_dt'][layer],
                                       w['ssm_b_re'][layer], w['ssm_b_im'][layer],
                                       w['ssm_c_re'][layer], w['ssm_c_im'][layer])
    head = jnp.zeros((2, LANE), F32)
    head = head.at[0, GDN_HEADS:2 * GDN_HEADS].set(w['gdn_a_log'][layer].astype(F32))
    head = head.at[1, GDN_HEADS:2 * GDN_HEADS].set(w['gdn_dt_bias'][layer].astype(F32))
    return dict(ar=ar, ai=ai, bblk=bblk, cblk=cblk,
                d=w['ssm_d'][layer].reshape(1, SSM_WIDTH).astype(F32),
                wglu=w['ssm_w_glu'][layer].astype(BF16),
                gdn_head=head,
                gdn_ng=w['gdn_norm_g'][layer].reshape(1, GDN_HEAD_DIM).astype(F32),
                conv_w=w['gdn_conv_w'][layer].astype(F32),
                sb_bias=w['sb_bias'][layer].astype(F32))


def _mixers_prompt(proj, lp, nb, seq):
    n = nb * seq
    nstream = nb * S5_SEGMENTS
    seg_len = seq // S5_SEGMENTS
    u3 = proj.reshape(nstream, seg_len, PROJ_W)
    zs = jnp.zeros((nstream, SSM_LANES), F32)
    zb = jnp.zeros((nb, SSM_LANES), F32)
    common = (lp['ar'], lp['ai'], lp['bblk'], lp['cblk'], lp['d'], lp['wglu'])
    er, ei = s5_scan(u3, CB_U, zs, zs, *common, lj=S5_STEPS_PER_CALL, n_valid=seg_len, emit_y=False)
    ir, ii, fr, fi = s5_combine(er, ei, zb, zb, lp['ar'], lp['ai'], nseg=S5_SEGMENTS, seg_len=seg_len)
    y_a3, _, _ = s5_scan(u3, CB_U, ir, ii, *common, lj=S5_STEPS_PER_CALL, n_valid=seg_len, emit_y=True)
    y_a = y_a3.reshape(n, SSM_WIDTH)
    y_b = sb_prompt(proj, lp['sb_bias'], nbatch=nb, seq=seq)
    x3 = proj.reshape(nb, seq, PROJ_W)
    cbuf8 = jnp.zeros((nb, SUBLANE, 3 * GDN_WIDTH), F32)
    s0 = jnp.zeros((nb, GDN_HEADS, GDN_HEAD_DIM, GDN_HEAD_DIM), F32)
    y_c3, s_fin, cnew = gdn(x3, CB_QKVC, cbuf8, lp['conv_w'], lp['gdn_head'], lp['gdn_ng'], s0, t_valid=seq)
    y_c = y_c3.reshape(n, GDN_WIDTH)
    news = (_col(proj, CB_K).reshape(nb, seq, SB_HEADS, SB_HEAD_DIM),
            _col(proj, CB_V).reshape(nb, seq, SB_HEADS, SB_HEAD_DIM),
            fr.reshape(nb, SSM_GROUPS, SSM_STATE), fi.reshape(nb, SSM_GROUPS, SSM_STATE),
            s_fin, cnew[:, SUBLANE - (GDN_CONV - 1):])
    return y_a, y_b, y_c, news


def _mixers_sample(proj, lp, nb, t, layer, cache_k4, cache_v4, page_table, ssm_re, ssm_im, s_delta, conv_buf):
    n = nb * t
    u3 = jnp.pad(_col(proj, CB_U).reshape(nb, t, SSM_WIDTH), ((0, 0), (0, SUBLANE - t), (0, 0)))
    y_a3, fr, fi = s5_scan(u3, 0, ssm_re.reshape(nb, SSM_LANES).astype(F32), ssm_im.reshape(nb, SSM_LANES).astype(F32),
                           lp['ar'], lp['ai'], lp['bblk'], lp['cblk'], lp['d'], lp['wglu'],
                           lj=SUBLANE, n_valid=t, emit_y=True)
    y_a = y_a3[:, :t].reshape(n, SSM_WIDTH)
    q4 = _col(proj, CB_Q).reshape(nb, t, SB_HEADS, SB_HEAD_DIM) * (SB_HEAD_DIM ** -0.5)
    q_s = jnp.pad(q4.transpose(0, 2, 1, 3), ((0, 0), (0, 0), (0, SB_QPAD - t), (0, 0)))
    k4 = _col(proj, CB_K).reshape(nb, t, SB_HEADS, SB_HEAD_DIM)
    v4 = _col(proj, CB_V).reshape(nb, t, SB_HEADS, SB_HEAD_DIM)
    as_page = lambda a: jnp.pad(a, ((0, 0), (0, PAGE_SIZE - t), (0, 0), (0, 0))).reshape(nb, PAGE_SIZE * SB_HEADS, SB_HEAD_DIM)
    bias_col = jnp.repeat(lp['sb_bias'], SB_QPAD).reshape(SB_HEADS * SB_QPAD, 1)
    o = sb_sample(q_s, as_page(k4), as_page(v4), cache_k4, cache_v4, page_table, bias_col, layer=layer)
    y_b = o.reshape(nb, SB_HEADS, SB_QPAD, SB_HEAD_DIM)[:, :, :t].transpose(0, 2, 1, 3).reshape(n, SB_WIDTH)
    x3 = jnp.pad(proj.reshape(nb, t, PROJ_W)[:, :, CB_QKVC * 512:], ((0, 0), (0, GDN_CHUNK - t), (0, 0)))
    cbuf8 = jnp.pad(conv_buf.astype(F32), ((0, 0), (SUBLANE - (GDN_CONV - 1), 0), (0, 0)))
    y_c3, s_fin, cnew = gdn(x3, 0, cbuf8, lp['conv_w'], lp['gdn_head'], lp['gdn_ng'], s_delta.astype(F32), t_valid=t)
    y_c = y_c3[:, :t].reshape(n, GDN_WIDTH)
    news = (k4, v4, fr.reshape(nb, SSM_GROUPS, SSM_STATE), fi.reshape(nb, SSM_GROUPS, SSM_STATE),
            s_fin, cnew[:, SUBLANE - (GDN_CONV - 1):])
    return y_a, y_b, y_c, news


def _run_trunk(x, mod, w, wb, final_norm_g, *, tm, mod_rows, tiles_per_group, mixers):
    nb, t, _ = x.shape
    x2 = x.reshape(nb * t, D_MODEL).astype(F32)
    dense = dict(tm=tm, mod_rows=mod_rows, tiles_per_group=tiles_per_group)
    news = []
    for layer in range(DEPTH):
        lp = _layer_params(layer, w)
        x2 = ffn(x2, mod, wb['norm_g'], wb['w_gu'], wb['w_down'], layer=layer, which=0, **dense)
        proj = in_proj(x2, mod, wb['norm_g'], wb['w_in'], layer=layer, **dense)
        y_a, y_b, y_c, new = mixers(proj, lp, layer)
        x2 = merge_out(x2, proj, y_a, y_b, y_c, mod, wb['w_branch'], wb['w_out'], layer=layer, **dense)
        x2 = ffn(x2, mod, wb['norm_g'], wb['w_gu'], wb['w_down'], layer=layer, which=1, **dense)
        news.append(new)
    y = final_norm(x2, final_norm_g.astype(F32), tm).reshape(nb, t, D_MODEL)
    stacked = [jnp.stack([nw[i] for nw in news]) for i in range(6)]
    return y, stacked


def kernel(x_prompt, x_sample, cache_k, cache_v, page_table, state_ssm_re, state_ssm_im, state_delta, state_conv,
           c_prompt, c_sample, w_ada, b_ada, norm_g, w_ffn_gu, w_ffn_down, w_in, sb_bias, ssm_lambda_re,
           ssm_lambda_im, ssm_log_dt, ssm_b_re, ssm_b_im, ssm_c_re, ssm_c_im, ssm_d, ssm_w_glu, gdn_conv_w,
           gdn_a_log, gdn_dt_bias, gdn_norm_g, w_branch, w_out, final_norm_g):
    w = dict(sb_bias=sb_bias, ssm_lambda_re=ssm_lambda_re, ssm_lambda_im=ssm_lambda_im, ssm_log_dt=ssm_log_dt,
             ssm_b_re=ssm_b_re, ssm_b_im=ssm_b_im, ssm_c_re=ssm_c_re, ssm_c_im=ssm_c_im, ssm_d=ssm_d,
             ssm_w_glu=ssm_w_glu, gdn_conv_w=gdn_conv_w, gdn_a_log=gdn_a_log, gdn_dt_bias=gdn_dt_bias,
             gdn_norm_g=gdn_norm_g)
    bp, seq, _ = x_prompt.shape
    bs, dec_t, _ = x_sample.shape
    depth = w_ada.shape[0]

    n_gate = 3 * D_MODEL
    split = 4 * 512 + 4 * GDN_WIDTH
    w_in_p = jnp.concatenate([w_in[:, :, split + 2 * GDN_HEADS:], w_in[:, :, :split],
                              w_in[:, :, split:split + 2 * GDN_HEADS],
                              jnp.zeros((depth, D_MODEL, 512 - 2 * GDN_HEADS), w_in.dtype)], axis=-1)
    assert w_in_p.shape[-1] == PROJ_W and w_in.shape[-1] - split - 2 * GDN_HEADS == n_gate
    wb = dict(norm_g=norm_g.reshape(depth, 3, 1, D_MODEL).astype(F32),
              w_gu=w_ffn_gu.astype(BF16), w_down=w_ffn_down.astype(BF16), w_in=w_in_p.astype(BF16),
              w_branch=w_branch.astype(BF16), w_out=w_out.astype(BF16))

    n_c = bp + bs
    pad_c = (-n_c) % SUBLANE
    c_all = jnp.pad(jnp.concatenate([c_prompt, c_sample], axis=0).astype(F32), ((0, pad_c), (0, 0)))
    mod = ada_mod(c_all, w_ada, b_ada.astype(F32))
    mod_p = mod[:, :bp].reshape(depth, bp, 1, 9 * D_MODEL)
    mod_s = jnp.repeat(mod[:, bp:n_c], dec_t, axis=1).reshape(depth, 1, bs * dec_t, 9 * D_MODEL)

    tm_p = 512
    y_prompt, new_p = _run_trunk(
        x_prompt, mod_p, w, wb, final_norm_g, tm=tm_p, mod_rows=1, tiles_per_group=seq // tm_p,
        mixers=lambda proj, lp, layer: _mixers_prompt(proj, lp, bp, seq))

    pool = cache_k.shape[1]
    cache_k4 = cache_k.reshape(depth, pool, PAGE_SIZE * SB_HEADS, SB_HEAD_DIM)
    cache_v4 = cache_v.reshape(depth, pool, PAGE_SIZE * SB_HEADS, SB_HEAD_DIM)
    y_sample, new_s = _run_trunk(
        x_sample, mod_s, w, wb, final_norm_g, tm=bs * dec_t, mod_rows=bs * dec_t, tiles_per_group=1,
        mixers=lambda proj, lp, layer: _mixers_sample(
            proj, lp, bs, dec_t, layer, cache_k4, cache_v4, page_table,
            state_ssm_re[layer], state_ssm_im[layer], state_delta[layer], state_conv[layer]))

    k_p, v_p, sr_p, si_p, d_p, c_p = new_p
    k_s, v_s, sr_s, si_s, d_s, c_s = new_s
    return (y_prompt, y_sample, k_p, v_p, k_s, v_s, sr_p, si_p, sr_s, si_s, d_p, d_s, c_p, c_s)
```
